```python
import jax, jax.numpy as jnp
from jax import lax
import numpy as np

D_MODEL = 1024
BATCH = 8
SEQ = 2048
DEPTH = 4
DEC_BATCH = 128
DEC_SEQ = 1
PAST_LEN = 2048
PAGE_SIZE = 128

N_A_LAYERS = DEPTH // 2
N_B_LAYERS = DEPTH - N_A_LAYERS
N_DENSE = (DEPTH + 1) // 2
N_MOE = DEPTH // 2
GLA_HEADS = 4
GLA_KEY_DIM = D_MODEL // 2
GLA_VAL_DIM = D_MODEL
GLA_DK = GLA_KEY_DIM // GLA_HEADS
GLA_DV = GLA_VAL_DIM // GLA_HEADS
GLA_GATE_RANK = 16
GLA_GATE_NORM = 16.0
GLA_CHUNK = 64
GLA_IN_DIM = 2 * GLA_KEY_DIM + 2 * GLA_VAL_DIM + GLA_GATE_RANK
SB_HEADS = 16
SB_HEAD_DIM = D_MODEL // SB_HEADS
SB_QBLOCK = 128
SB_BIAS_INIT = -7.0
D_FF = 2816
N_EXPERTS = 8
TOP_K = 2
D_FF_EXPERT = 3584
EPS = 1e-6

kernel_name = "yoco_gla_stickbreaking_adaln_decoder_step"


def rmsnorm(x, g):
    xf = x.astype(jnp.float32)
    y = xf * lax.rsqrt(jnp.mean(xf * xf, axis=-1, keepdims=True) + EPS)
    return (y * g.astype(jnp.float32)).astype(x.dtype)


def modulate(h, shift, scale):
    return h * (1 + scale[:, None, :]) + shift[:, None, :]


def gla_scan(q, k, v, log_a, s0):
    B, L = q.shape[:2]
    C = min(GLA_CHUNK, L)
    n = -(-L // C)
    pad = n * C - L

    def prep(t):
        t = jnp.pad(t.astype(jnp.float32), ((0, 0), (0, pad), (0, 0), (0, 0)))
        return t.reshape(B, n, C, t.shape[2], t.shape[3]).swapaxes(0, 1)

    qc, kc, vc, ac = prep(q), prep(k), prep(v), prep(log_a)
    causal = jnp.tril(jnp.ones((C, C), dtype=bool))[None, :, :, None, None]

    def step(S, inp):
        qi, ki, vi, ai = inp
        b = jnp.cumsum(ai, axis=1)
        o_inter = jnp.einsum('bthk,bhkv->bthv', qi * jnp.exp(b), S)
        diff = b[:, :, None] - b[:, None, :]
        decay = jnp.exp(jnp.where(causal, diff, -jnp.inf))
        scores = jnp.einsum('bthk,bshk,btshk->bhts', qi, ki, decay)
        o_intra = jnp.einsum('bhts,bshv->bthv', scores, vi)
        b_last = b[:, -1]
        S = jnp.exp(b_last)[..., None] * S + jnp.einsum('bshk,bshv->bhkv', ki * jnp.exp(b_last[:, None] - b), vi)
        return S, o_inter + o_intra

    S, o = lax.scan(step, s0.astype(jnp.float32), (qc, kc, vc, ac))
    o = o.swapaxes(0, 1).reshape(B, n * C, o.shape[3], o.shape[4])[:, :L]
    return o, S


def gla_mixer(h, s0, w_in, w_gate2, b_gate, g_norm, w_out):
    B, L, _ = h.shape
    proj = h @ w_in
    q, k, v, g, r = jnp.split(proj, [GLA_KEY_DIM, 2 * GLA_KEY_DIM, 2 * GLA_KEY_DIM + GLA_VAL_DIM,
                                     2 * GLA_KEY_DIM + 2 * GLA_VAL_DIM], axis=-1)
    log_a = jax.nn.log_sigmoid((r @ w_gate2 + b_gate).astype(jnp.float32)) / GLA_GATE_NORM
    q = q.reshape(B, L, GLA_HEADS, GLA_DK) * (GLA_DK ** -0.5)
    k = k.reshape(B, L, GLA_HEADS, GLA_DK)
    v = v.reshape(B, L, GLA_HEADS, GLA_DV)
    o, S = gla_scan(q, k, v, log_a.reshape(B, L, GLA_HEADS, GLA_DK), s0)
    o = rmsnorm(o, g_norm).reshape(B, L, GLA_VAL_DIM).astype(h.dtype) * jax.nn.silu(g)
    return o @ w_out, S


def stick_breaking_attention(q, k, v, bias, q_offset):
    Lq = q.shape[1]
    bias_f = bias.astype(jnp.float32)[None, :, None, None]
    outs = []
    for start in range(0, Lq, SB_QBLOCK):
        stop = min(start + SB_QBLOCK, Lq)
        n_keys = q_offset + stop - 1
        qb = q[:, start:stop].astype(jnp.float32)
        kb = k[:, :n_keys].astype(jnp.float32)
        vb = v[:, :n_keys].astype(jnp.float32)
        z = jnp.einsum('bqhd,bkhd->bhqk', qb, kb) * (SB_HEAD_DIM ** -0.5) + bias_f
        t_pos = q_offset + jnp.arange(start, stop)
        s_pos = jnp.arange(n_keys)
        mask = s_pos[None, :] < t_pos[:, None]
        log1m = jnp.where(mask, jax.nn.log_sigmoid(-z), 0.0)
        later = lax.cumsum(log1m, axis=3, reverse=True) - log1m
        log_att = jnp.where(mask, log1m + z + later, -jnp.inf)
        outs.append(jnp.einsum('bhqk,bkhd->bqhd', jnp.exp(log_att), vb))
    return jnp.concatenate(outs, axis=1).astype(q.dtype)


def sb_mixer(h, k_all, v_all, q_offset, w_q, b_sb, w_out):
    B, L, _ = h.shape
    q = (h @ w_q).reshape(B, L, SB_HEADS, SB_HEAD_DIM)
    o = stick_breaking_attention(q, k_all, v_all, b_sb, q_offset)
    return o.reshape(B, L, D_MODEL) @ w_out


def shared_kv(x, c, kv_norm, kv_ada_w, kv_ada_b, kv_w):
    B, L, _ = x.shape
    shift, scale = jnp.split(jax.nn.silu(c) @ kv_ada_w + kv_ada_b, 2, axis=-1)
    h = modulate(rmsnorm(x, kv_norm), shift, scale)
    k, v = jnp.split(h @ kv_w, 2, axis=-1)
    return k.reshape(B, L, SB_HEADS, SB_HEAD_DIM), v.reshape(B, L, SB_HEADS, SB_HEAD_DIM)


def swiglu(h, w_gate_up, w_down):
    a, b = jnp.split(h @ w_gate_up, 2, axis=-1)
    return (jax.nn.silu(a) * b) @ w_down


def moe_swiglu(h, w_router, b_router, w_gate_up, w_down):
    logits = (h @ w_router).astype(jnp.float32) + b_router.astype(jnp.float32)
    top_val, top_idx = lax.top_k(logits, TOP_K)
    top_w = jax.nn.softmax(top_val, axis=-1)
    gates = jnp.sum(jax.nn.one_hot(top_idx, N_EXPERTS, dtype=jnp.float32) * top_w[..., None], axis=-2)
    gates = gates.astype(h.dtype)
    y = jnp.zeros_like(h)
    for e in range(N_EXPERTS):
        y = y + gates[..., e:e + 1] * swiglu(h, w_gate_up[e], w_down[e])
    return y


def trunk(x, c, gla_s0, past_k, past_v, p):
    q_offset = past_k.shape[1]
    new_states = []
    k_all = v_all = new_k = new_v = None
    for layer in range(DEPTH):
        sh1, sc1, g1, sh2, sc2, g2 = jnp.split(jax.nn.silu(c) @ p['ada_w'][layer] + p['ada_b'][layer], 6, axis=-1)
        h = modulate(rmsnorm(x, p['norm_mix'][layer]), sh1, sc1)
        if layer < N_A_LAYERS:
            out, S = gla_mixer(h, gla_s0[layer], p['gla_w_in'][layer], p['gla_w_gate2'][layer],
                               p['gla_b_gate'][layer], p['gla_norm'][layer], p['gla_w_out'][layer])
            new_states.append(S)
        else:
            bi = layer - N_A_LAYERS
            out = sb_mixer(h, k_all, v_all, q_offset, p['sb_w_q'][bi], p['sb_b'][bi], p['sb_w_out'][bi])
        x = x + g1[:, None, :] * out
        h = modulate(rmsnorm(x, p['norm_ffn'][layer]), sh2, sc2)
        if layer % 2 == 0:
            f = swiglu(h, p['ffn_w_gate_up'][layer // 2], p['ffn_w_down'][layer // 2])
        else:
            f = moe_swiglu(h, p['moe_w_router'][layer // 2], p['moe_b_router'][layer // 2],
                           p['moe_w_gate_up'][layer // 2], p['moe_w_down'][layer // 2])
        x = x + g2[:, None, :] * f
        if layer == N_A_LAYERS - 1:
            new_k, new_v = shared_kv(x, c, p['kv_norm'], p['kv_ada_w'], p['kv_ada_b'], p['kv_w'])
            k_all = jnp.concatenate([past_k.astype(new_k.dtype), new_k], axis=1)
            v_all = jnp.concatenate([past_v.astype(new_v.dtype), new_v], axis=1)
    shf, scf = jnp.split(jax.nn.silu(c) @ p['final_ada_w'] + p['final_ada_b'], 2, axis=-1)
    y = modulate(rmsnorm(x, p['final_norm']), shf, scf)
    return y, new_k, new_v, jnp.stack(new_states)


def setup_inputs(seed: int = 0) -> dict:
    key = jax.random.key(seed)
    ks = iter(jax.random.split(key, 40))

    def nrm(shape, scale):
        return jax.random.normal(next(ks), shape, jnp.float32) * scale

    def gain(shape):
        return 1.0 + nrm(shape, 0.02)

    n_pages = PAST_LEN // PAGE_SIZE
    n_pool = (DEC_BATCH * n_pages * 5) // 4
    d = D_MODEL
    out = {}
    out['x_prompt'] = nrm((BATCH, SEQ, d), 1.0)
    out['x_sample'] = nrm((DEC_BATCH, DEC_SEQ, d), 1.0)
    out['cache_k'] = nrm((n_pool, PAGE_SIZE, SB_HEADS, SB_HEAD_DIM), 1.0)
    out['cache_v'] = nrm((n_pool, PAGE_SIZE, SB_HEADS, SB_HEAD_DIM), 1.0)
    out['state_gla'] = nrm((N_A_LAYERS, DEC_BATCH, GLA_HEADS, GLA_DK, GLA_DV), 0.5)
    out['page_table'] = jax.random.permutation(next(ks), n_pool)[:DEC_BATCH * n_pages].reshape(
        DEC_BATCH, n_pages).astype(jnp.int32)
    out['c_prompt'] = nrm((BATCH, d), 1.0)
    out['c_sample'] = nrm((DEC_BATCH, d), 1.0)
    out['ada_w'] = nrm((DEPTH, d, 6 * d), 0.5 * d ** -0.5)
    out['ada_b'] = nrm((DEPTH, 6 * d), 0.02)
    out['norm_mix'] = gain((DEPTH, d))
    out['norm_ffn'] = gain((DEPTH, d))
    out['gla_w_in'] = nrm((N_A_LAYERS, d, GLA_IN_DIM), d ** -0.5)
    out['gla_w_gate2'] = nrm((N_A_LAYERS, GLA_GATE_RANK, GLA_KEY_DIM), GLA_GATE_RANK ** -0.5)
    out['gla_b_gate'] = nrm((N_A_LAYERS, GLA_KEY_DIM), 0.02)
    out['gla_norm'] = gain((N_A_LAYERS, GLA_DV))
    out['gla_w_out'] = nrm((N_A_LAYERS, GLA_VAL_DIM, d), GLA_VAL_DIM ** -0.5)
    out['kv_norm'] = gain((d,))
    out['kv_ada_w'] = nrm((d, 2 * d), 0.5 * d ** -0.5)
    out['kv_ada_b'] = nrm((2 * d,), 0.02)
    out['kv_w'] = nrm((d, 2 * d), d ** -0.5)
    out['sb_w_q'] = nrm((N_B_LAYERS, d, d), d ** -0.5)
    out['sb_b'] = SB_BIAS_INIT + nrm((N_B_LAYERS, SB_HEADS), 0.1)
    out['sb_w_out'] = nrm((N_B_LAYERS, d, d), d ** -0.5)
    out['ffn_w_gate_up'] = nrm((N_DENSE, d, 2 * D_FF), d ** -0.5)
    out['ffn_w_down'] = nrm((N_DENSE, D_FF, d), D_FF ** -0.5)
    out['moe_w_router'] = nrm((N_MOE, d, N_EXPERTS), d ** -0.5)
    out['moe_b_router'] = nrm((N_MOE, N_EXPERTS), 0.01)
    out['moe_w_gate_up'] = nrm((N_MOE, N_EXPERTS, d, 2 * D_FF_EXPERT), d ** -0.5)
    out['moe_w_down'] = nrm((N_MOE, N_EXPERTS, D_FF_EXPERT, d), D_FF_EXPERT ** -0.5)
    out['final_norm'] = gain((d,))
    out['final_ada_w'] = nrm((d, 2 * d), 0.5 * d ** -0.5)
    out['final_ada_b'] = nrm((2 * d,), 0.02)
    return out


def reference(x_prompt, x_sample, cache_k, cache_v, state_gla, page_table, c_prompt, c_sample,
              ada_w, ada_b, norm_mix, norm_ffn, gla_w_in, gla_w_gate2, gla_b_gate, gla_norm, gla_w_out,
              kv_norm, kv_ada_w, kv_ada_b, kv_w, sb_w_q, sb_b, sb_w_out, ffn_w_gate_up, ffn_w_down,
              moe_w_router, moe_b_router, moe_w_gate_up, moe_w_down, final_norm, final_ada_w, final_ada_b):
    p = dict(ada_w=ada_w, ada_b=ada_b, norm_mix=norm_mix, norm_ffn=norm_ffn, gla_w_in=gla_w_in,
             gla_w_gate2=gla_w_gate2, gla_b_gate=gla_b_gate, gla_norm=gla_norm, gla_w_out=gla_w_out,
             kv_norm=kv_norm, kv_ada_w=kv_ada_w, kv_ada_b=kv_ada_b, kv_w=kv_w, sb_w_q=sb_w_q, sb_b=sb_b,
             sb_w_out=sb_w_out, ffn_w_gate_up=ffn_w_gate_up, ffn_w_down=ffn_w_down,
             moe_w_router=moe_w_router, moe_b_router=moe_b_router, moe_w_gate_up=moe_w_gate_up,
             moe_w_down=moe_w_down, final_norm=final_norm, final_ada_w=final_ada_w, final_ada_b=final_ada_b)
    b_p = x_prompt.shape[0]
    gla0 = jnp.zeros((N_A_LAYERS, b_p, GLA_HEADS, GLA_DK, GLA_DV), jnp.float32)
    empty = jnp.zeros((b_p, 0, SB_HEADS, SB_HEAD_DIM), x_prompt.dtype)
    y_prompt, k_prompt, v_prompt, gla_prompt = trunk(x_prompt, c_prompt, gla0, empty, empty, p)
    b_s = x_sample.shape[0]
    past_k = cache_k[page_table].reshape(b_s, -1, SB_HEADS, SB_HEAD_DIM)
    past_v = cache_v[page_table].reshape(b_s, -1, SB_HEADS, SB_HEAD_DIM)
    y_sample, k_sample, v_sample, gla_sample = trunk(x_sample, c_sample, state_gla, past_k, past_v, p)
    return (y_prompt, y_sample, k_prompt, v_prompt, gla_prompt, k_sample, v_sample, gla_sample)
```

```python
import functools

import jax
import jax.numpy as jnp
from jax import lax
from jax.experimental import pallas as pl
from jax.experimental.pallas import tpu as pltpu

F32 = jnp.float32
BF16 = jnp.bfloat16
EPS = 1e-6

LANES = 128
SUBLANES = 8
VMEM_LIMIT_BYTES = 56 * 1024 * 1024

GLA_HEADS = 4
GLA_GATE_NORM = 16.0
GLA_CHUNK = 64
GLA_SUB = 16
SB_HEADS = 16
SB_HEAD_DIM = 64
N_EXPERTS = 8
PAGES_PER_STEP = 8


def _params(*sem):
    return pltpu.CompilerParams(dimension_semantics=sem, vmem_limit_bytes=VMEM_LIMIT_BYTES)


def _bdot(a, b):
    return jnp.dot(a.astype(BF16), b.astype(BF16), preferred_element_type=F32)


def _bdot_nt(a, b):
    return lax.dot_general(a.astype(BF16), b.astype(BF16), (((1,), (1,)), ((), ())),
                           preferred_element_type=F32)


def _bdot_tn(a, b):
    return lax.dot_general(a.astype(BF16), b.astype(BF16), (((0,), (0,)), ((), ())),
                           preferred_element_type=F32)


def _split_dot(x, m, parts):
    out = None
    rem = x
    for p in range(parts):
        hi = rem.astype(BF16)
        t = jnp.dot(hi, m, preferred_element_type=F32)
        out = t if out is None else out + t
        if p + 1 < parts:
            rem = rem - hi.astype(F32)
    return out


def _silu(x):
    return x * (1.0 / (1.0 + jnp.exp(-x)))


def _softplus(z):
    return jnp.maximum(z, 0.0) + jnp.log1p(jnp.exp(-jnp.abs(z)))


def _normmod(x, g, shift, scale):
    y = x * lax.rsqrt(jnp.mean(x * x, axis=-1, keepdims=True) + EPS)
    return (y * g) * (1.0 + scale) + shift


def _mod_spec(mod, rows_per_seq, tm, col_block=None, ncol_args=1):
    s, r, d = mod.shape
    cb = d if col_block is None else col_block
    if r == 1:
        per = rows_per_seq // tm
        if ncol_args == 1:
            return pl.BlockSpec((None, 1, cb), lambda i: (i // per, 0, 0))
        return pl.BlockSpec((None, 1, cb), lambda i, j: (i // per, 0, j if col_block else 0))
    if ncol_args == 1:
        return pl.BlockSpec((None, tm, cb), lambda i: (0, i, 0))
    return pl.BlockSpec((None, tm, cb), lambda i, j: (0, i, j if col_block else 0))


def _ada_kernel(c_ref, w_ref, b_ref, o_ref):
    o_ref[...] = _bdot(_silu(c_ref[...]), w_ref[...]) + b_ref[...]


def ada_matmul(c, w, b):
    g, d, n = w.shape
    r = c.shape[0]
    tn = 1024
    return pl.pallas_call(
        _ada_kernel,
        grid=(g, n // tn),
        in_specs=[pl.BlockSpec((r, d), lambda l, j: (0, 0)),
                  pl.BlockSpec((None, d, tn), lambda l, j: (l, 0, j)),
                  pl.BlockSpec((None, 1, tn), lambda l, j: (l, 0, j))],
        out_specs=pl.BlockSpec((None, r, tn), lambda l, j: (l, 0, j)),
        out_shape=jax.ShapeDtypeStruct((g, r, n), F32),
        compiler_params=_params("parallel", "parallel"),
        name="ada_matmul",
    )(c, w, b.reshape(g, 1, n))


def _nm_mm_kernel(x_ref, g_ref, sh_ref, sc_ref, w_ref, o_ref, h_scr, *, out_scale):
    @pl.when(pl.program_id(1) == 0)
    def _():
        h_scr[...] = _normmod(x_ref[...], g_ref[...], sh_ref[...], sc_ref[...]).astype(BF16)

    acc = jnp.dot(h_scr[...], w_ref[...].astype(BF16), preferred_element_type=F32)
    o_ref[...] = acc * out_scale if out_scale != 1.0 else acc


def normmod_matmul(x, g, shift, scale, w, *, rows_per_seq, tm, tn, out_scale=1.0):
    t, d = x.shape
    n = w.shape[1]
    return pl.pallas_call(
        functools.partial(_nm_mm_kernel, out_scale=out_scale),
        grid=(t // tm, n // tn),
        in_specs=[pl.BlockSpec((tm, d), lambda i, j: (i, 0)),
                  pl.BlockSpec((1, d), lambda i, j: (0, 0)),
                  _mod_spec(shift, rows_per_seq, tm, ncol_args=2),
                  _mod_spec(scale, rows_per_seq, tm, ncol_args=2),
                  pl.BlockSpec((d, tn), lambda i, j: (0, j))],
        out_specs=pl.BlockSpec((tm, tn), lambda i, j: (i, j)),
        out_shape=jax.ShapeDtypeStruct((t, n), F32),
        scratch_shapes=[pltpu.VMEM((tm, d), BF16)],
        compiler_params=_params("parallel", "arbitrary"),
        name="normmod_matmul",
    )(x, g.reshape(1, d), shift, scale, w)


def _nm_mm_t_kernel(x_ref, g_ref, sh_ref, sc_ref, w_ref, o_ref, ot_ref, h_scr):
    @pl.when(pl.program_id(1) == 0)
    def _():
        h_scr[...] = _normmod(x_ref[...], g_ref[...], sh_ref[...], sc_ref[...]).astype(BF16)

    acc = jnp.dot(h_scr[...], w_ref[...].astype(BF16), preferred_element_type=F32)
    o_ref[...] = acc
    ot_ref[...] = acc.T


def normmod_matmul_t(x, g, shift, scale, w, *, rows_per_seq, tm, tn):
    t, d = x.shape
    n = w.shape[1]
    per = rows_per_seq // tm
    return pl.pallas_call(
        _nm_mm_t_kernel,
        grid=(t // tm, n // tn),
        in_specs=[pl.BlockSpec((tm, d), lambda i, j: (i, 0)),
                  pl.BlockSpec((1, d), lambda i, j: (0, 0)),
                  _mod_spec(shift, rows_per_seq, tm, ncol_args=2),
                  _mod_spec(scale, rows_per_seq, tm, ncol_args=2),
                  pl.BlockSpec((d, tn), lambda i, j: (0, j))],
        out_specs=[pl.BlockSpec((tm, tn), lambda i, j: (i, j)),
                   pl.BlockSpec((None, tn, tm), lambda i, j: (i // per, j, i % per))],
        out_shape=[jax.ShapeDtypeStruct((t, n), F32),
                   jax.ShapeDtypeStruct((t // rows_per_seq, n, rows_per_seq), F32)],
        scratch_shapes=[pltpu.VMEM((tm, d), BF16)],
        compiler_params=_params("parallel", "arbitrary"),
        name="normmod_matmul_t",
    )(x, g.reshape(1, d), shift, scale, w)


def _nm_kernel(x_ref, g_ref, sh_ref, sc_ref, o_ref):
    o_ref[...] = _normmod(x_ref[...], g_ref[...], sh_ref[...], sc_ref[...])


def normmod(x, g, shift, scale, *, rows_per_seq, tm):
    t, d = x.shape
    return pl.pallas_call(
        _nm_kernel,
        grid=(t // tm,),
        in_specs=[pl.BlockSpec((tm, d), lambda i: (i, 0)),
                  pl.BlockSpec((1, d), lambda i: (0, 0)),
                  _mod_spec(shift, rows_per_seq, tm),
                  _mod_spec(scale, rows_per_seq, tm)],
        out_specs=pl.BlockSpec((tm, d), lambda i: (i, 0)),
        out_shape=jax.ShapeDtypeStruct((t, d), F32),
        compiler_params=_params("parallel"),
        name="normmod",
    )(x, g.reshape(1, d), shift, scale)


def _mm_res_kernel(a_ref, w_ref, x_ref, gt_ref, o_ref):
    o_ref[...] = x_ref[...] + gt_ref[...] * _bdot(a_ref[...], w_ref[...])


def matmul_residual(a, w, x, gate, *, rows_per_seq, tm, tn):
    t, k = a.shape
    n = w.shape[1]
    return pl.pallas_call(
        _mm_res_kernel,
        grid=(t // tm, n // tn),
        in_specs=[pl.BlockSpec((tm, k), lambda i, j: (i, 0)),
                  pl.BlockSpec((k, tn), lambda i, j: (0, j)),
                  pl.BlockSpec((tm, tn), lambda i, j: (i, j)),
                  _mod_spec(gate, rows_per_seq, tm, col_block=tn, ncol_args=2)],
        out_specs=pl.BlockSpec((tm, tn), lambda i, j: (i, j)),
        out_shape=jax.ShapeDtypeStruct((t, n), F32),
        compiler_params=_params("parallel", "parallel"),
        name="matmul_residual",
    )(a, w, x, gate)


def _ffn_kernel(x_ref, g_ref, sh_ref, sc_ref, gt_ref, wg_ref, wu_ref, wd_ref, o_ref, h_scr, acc_scr):
    f = pl.program_id(1)

    @pl.when(f == 0)
    def _():
        h_scr[...] = _normmod(x_ref[...], g_ref[...], sh_ref[...], sc_ref[...]).astype(BF16)
        acc_scr[...] = jnp.zeros_like(acc_scr)

    h = h_scr[...]
    a = jnp.dot(h, wg_ref[...].astype(BF16), preferred_element_type=F32)
    b = jnp.dot(h, wu_ref[...].astype(BF16), preferred_element_type=F32)
    acc_scr[...] += _bdot(_silu(a) * b, wd_ref[...])

    @pl.when(f == pl.num_programs(1) - 1)
    def _():
        o_ref[...] = x_ref[...] + gt_ref[...] * acc_scr[...]


def ffn_block(x, g, shift, scale, gate, w_gate_up, w_down, *, rows_per_seq, tm, tf):
    t, d = x.shape
    ff = w_down.shape[0]
    nf = ff // tf
    return pl.pallas_call(
        _ffn_kernel,
        grid=(t // tm, nf),
        in_specs=[pl.BlockSpec((tm, d), lambda i, f: (i, 0)),
                  pl.BlockSpec((1, d), lambda i, f: (0, 0)),
                  _mod_spec(shift, rows_per_seq, tm, ncol_args=2),
                  _mod_spec(scale, rows_per_seq, tm, ncol_args=2),
                  _mod_spec(gate, rows_per_seq, tm, ncol_args=2),
                  pl.BlockSpec((d, tf), lambda i, f: (0, f)),
                  pl.BlockSpec((d, tf), lambda i, f: (0, nf + f)),
                  pl.BlockSpec((tf, d), lambda i, f: (f, 0))],
        out_specs=pl.BlockSpec((tm, d), lambda i, f: (i, 0)),
        out_shape=jax.ShapeDtypeStruct((t, d), F32),
        scratch_shapes=[pltpu.VMEM((tm, d), BF16), pltpu.VMEM((tm, d), F32)],
        compiler_params=_params("parallel", "arbitrary"),
        name="ffn_block",
    )(x, g.reshape(1, d), shift, scale, gate, w_gate_up, w_gate_up, w_down)


def _gla_kernel(*refs, chunk, n_chunks, valid_len, has_s0, q_scale):
    if has_s0:
        (q_ref, k_ref, v_ref, g_ref, r_ref, wg2_ref, bg_ref, gn_ref, s0_ref,
         y_ref, so_ref, st_scr) = refs
    else:
        (q_ref, k_ref, v_ref, g_ref, r_ref, wg2_ref, bg_ref, gn_ref,
         y_ref, so_ref, st_scr) = refs
        s0_ref = None
    t_idx = pl.program_id(2)
    rows = chunk * n_chunks

    @pl.when(t_idx == 0)
    def _():
        if has_s0:
            st_scr[...] = s0_ref[...].T
        else:
            st_scr[...] = jnp.zeros_like(st_scr)

    r_i = lax.broadcasted_iota(jnp.int32, (chunk, chunk), 0)
    c_i = lax.broadcasted_iota(jnp.int32, (chunk, chunk), 1)
    tril = jnp.where(c_i <= r_i, 1.0, 0.0).astype(BF16)
    sub_r = lax.broadcasted_iota(jnp.int32, (GLA_SUB, 1), 0)
    sub_c = lax.broadcasted_iota(jnp.int32, (GLA_SUB, GLA_SUB), 1)
    wg2 = wg2_ref[...]
    bg = bg_ref[...]
    gn = gn_ref[...]
    n_sub = chunk // GLA_SUB

    for ci in range(n_chunks):
        r0 = ci * chunk
        q = q_ref[r0:r0 + chunk, :] * q_scale
        k = k_ref[r0:r0 + chunk, :]
        v = v_ref[r0:r0 + chunk, :]
        pre = _bdot(r_ref[r0:r0 + chunk, :], wg2) + bg
        la = (jnp.minimum(pre, 0.0) - jnp.log1p(jnp.exp(-jnp.abs(pre)))) / GLA_GATE_NORM
        if valid_len is not None:
            row = lax.broadcasted_iota(jnp.int32, (chunk, 1), 0) + (t_idx * rows + r0)
            la = jnp.where(row < valid_len, la, 0.0)
        b = _split_dot_left(tril, la)
        st = st_scr[...]
        o_inter = _bdot_nt(q * jnp.exp(b), st)
        for i in range(n_sub):
            s0 = i * GLA_SUB
            bi = b[s0:s0 + GLA_SUB, :]
            qi = q[s0:s0 + GLA_SUB, :]
            ki = k[s0:s0 + GLA_SUB, :]
            dii = jnp.zeros((GLA_SUB, GLA_SUB), F32)
            for s in range(GLA_SUB):
                dec = jnp.exp(jnp.where(sub_r >= s, bi - bi[s:s + 1, :], -jnp.inf))
                col = jnp.sum(qi * ki[s:s + 1, :] * dec, axis=-1, keepdims=True)
                dii = jnp.where(sub_c == s, col, dii)
            oi = o_inter[s0:s0 + GLA_SUB, :] + _bdot(dii, v[s0:s0 + GLA_SUB, :])
            if i > 0:
                bref = b[s0 - 1:s0, :]
                qs = qi * jnp.exp(bi - bref)
                ks = k[0:s0, :] * jnp.exp(bref - b[0:s0, :])
                oi = oi + _bdot(_bdot_nt(qs, ks), v[0:s0, :])
            gi = g_ref[r0 + s0:r0 + s0 + GLA_SUB, :]
            yn = oi * lax.rsqrt(jnp.mean(oi * oi, axis=-1, keepdims=True) + EPS) * gn
            y_ref[r0 + s0:r0 + s0 + GLA_SUB, :] = yn * _silu(gi)
        b_last = b[chunk - 1:chunk, :]
        kt = k * jnp.exp(b_last - b)
        st_scr[...] = st * jnp.exp(b_last) + _bdot_tn(v, kt)

    @pl.when(t_idx == pl.num_programs(2) - 1)
    def _():
        so_ref[...] = st_scr[...].T


def _split_dot_left(m, x):
    hi = x.astype(BF16)
    r1 = x - hi.astype(F32)
    mid = r1.astype(BF16)
    lo = (r1 - mid.astype(F32)).astype(BF16)
    return (jnp.dot(m, hi, preferred_element_type=F32) + jnp.dot(m, mid, preferred_element_type=F32)
            + jnp.dot(m, lo, preferred_element_type=F32))


def gla_scan(proj, w_gate2, b_gate, g_norm, s0, *, chunk, n_chunks, valid_len):
    bsz, length, _ = proj.shape
    dk = w_gate2.shape[1] // GLA_HEADS
    dv = g_norm.shape[0]
    rows = chunk * n_chunks
    wg2 = jnp.zeros((LANES, w_gate2.shape[1]), F32).at[:w_gate2.shape[0]].set(w_gate2)
    nkb = GLA_HEADS
    in_specs = [pl.BlockSpec((None, rows, dk), lambda b, h, t: (b, t, h)),
                pl.BlockSpec((None, rows, dk), lambda b, h, t: (b, t, nkb + h)),
                pl.BlockSpec((None, rows, dv), lambda b, h, t: (b, t, GLA_HEADS + h)),
                pl.BlockSpec((None, rows, dv), lambda b, h, t: (b, t, 2 * GLA_HEADS + h)),
                pl.BlockSpec((None, rows, LANES), lambda b, h, t: (b, t, (2 * GLA_HEADS * dk + 2 * GLA_HEADS * dv) // LANES)),
                pl.BlockSpec((LANES, dk), lambda b, h, t: (0, h)),
                pl.BlockSpec((1, dk), lambda b, h, t: (0, h)),
                pl.BlockSpec((1, dv), lambda b, h, t: (0, 0))]
    args = [proj, proj, proj, proj, proj, wg2, b_gate.reshape(1, -1), g_norm.reshape(1, dv)]
    if s0 is not None:
        in_specs.append(pl.BlockSpec((None, None, dk, dv), lambda b, h, t: (b, h, 0, 0)))
        args.append(s0)
    return pl.pallas_call(
        functools.partial(_gla_kernel, chunk=chunk, n_chunks=n_chunks, valid_len=valid_len,
                          has_s0=s0 is not None, q_scale=float(dk) ** -0.5),
        grid=(bsz, GLA_HEADS, length // rows),
        in_specs=in_specs,
        out_specs=[pl.BlockSpec((None, rows, dv), lambda b, h, t: (b, t, h)),
                   pl.BlockSpec((None, None, dk, dv), lambda b, h, t: (b, h, 0, 0))],
        out_shape=[jax.ShapeDtypeStruct((bsz, length, GLA_HEADS * dv), F32),
                   jax.ShapeDtypeStruct((bsz, GLA_HEADS, dk, dv), F32)],
        scratch_shapes=[pltpu.VMEM((dv, dk), F32)],
        compiler_params=_params("parallel", "parallel", "arbitrary"),
        name="gla_scan",
    )(*args)


def _sb_kernel(bias_ref, q_ref, k_ref, v_ref, u_ref, o_ref, acc_scr, c_scr, *, tq):
    hp = pl.program_id(1)
    qi = pl.program_id(2)
    lane = lax.broadcasted_iota(jnp.int32, (tq, LANES), 1)
    first = lane < SB_HEAD_DIM
    qf = q_ref[...]
    qm = (jnp.where(first, qf, 0.0).astype(BF16), jnp.where(first, 0.0, qf).astype(BF16))
    bias = (bias_ref[2 * hp], bias_ref[2 * hp + 1])
    u = u_ref[...]
    acc_scr[...] = jnp.zeros_like(acc_scr)
    c_scr[...] = jnp.zeros_like(c_scr)
    row_i = lax.broadcasted_iota(jnp.int32, (tq, tq), 0)
    col_i = lax.broadcasted_iota(jnp.int32, (tq, tq), 1)
    strict = col_i < row_i

    def block(j, masked):
        start = pl.multiple_of(j * tq, tq)
        kb = k_ref[pl.ds(start, tq), :].astype(BF16)
        vb = v_ref[pl.ds(start, tq), :].astype(BF16)
        for hh in range(2):
            z = lax.dot_general(qm[hh], kb, (((1,), (1,)), ((), ())), preferred_element_type=F32) + bias[hh]
            sp = _softplus(z)
            lb = z - sp
            if masked:
                sp = jnp.where(strict, sp, 0.0)
            cs = _split_dot(sp, u, 2)
            c = c_scr[hh]
            att = jnp.exp(lb - cs - c[:, 0:1])
            if masked:
                att = jnp.where(strict, att, 0.0)
            acc_scr[hh] += jnp.dot(att.astype(BF16), vb, preferred_element_type=F32)
            c_scr[hh] = c + jnp.sum(sp, axis=-1, keepdims=True)

    block(qi, True)

    def body(n, carry):
        block(qi - 1 - n, False)
        return carry

    lax.fori_loop(0, qi, body, 0)
    o_ref[...] = jnp.where(first, acc_scr[0], acc_scr[1])


def sb_attention_prompt(q, k, v, bias, *, tq):
    bsz, length, d = q.shape
    r = lax.broadcasted_iota(jnp.int32, (tq, tq), 0)
    c = lax.broadcasted_iota(jnp.int32, (tq, tq), 1)
    u = jnp.where(r > c, 1.0, 0.0).astype(BF16)
    grid_spec = pltpu.PrefetchScalarGridSpec(
        num_scalar_prefetch=1,
        grid=(bsz, d // LANES, length // tq),
        in_specs=[pl.BlockSpec((None, tq, LANES), lambda b, h, i, bias: (b, i, h)),
                  pl.BlockSpec((None, length, LANES), lambda b, h, i, bias: (b, 0, h)),
                  pl.BlockSpec((None, length, LANES), lambda b, h, i, bias: (b, 0, h)),
                  pl.BlockSpec((tq, tq), lambda b, h, i, bias: (0, 0))],
        out_specs=pl.BlockSpec((None, tq, LANES), lambda b, h, i, bias: (b, i, h)),
        scratch_shapes=[pltpu.VMEM((2, tq, LANES), F32), pltpu.VMEM((2, tq, LANES), F32)],
    )
    return pl.pallas_call(
        functools.partial(_sb_kernel, tq=tq),
        grid_spec=grid_spec,
        out_shape=jax.ShapeDtypeStruct((bsz, length, d), F32),
        compiler_params=_params("parallel", "parallel", "parallel"),
        name="sb_attention_prompt",
    )(bias, q, k, v, u)


def _dec_z_kernel(pt_ref, q_ref, bias_ref, *rest):
    k_refs, o_ref = rest[:-1], rest[-1]
    qt = q_ref[...]
    page = k_refs[0].shape[-1]
    qb = [jnp.broadcast_to(qt[:, h:h + 1], (SB_HEAD_DIM, page)) for h in range(SB_HEADS)]
    for p, k_ref in enumerate(k_refs):
        for h in range(SB_HEADS):
            zrow = jnp.sum(k_ref[h] * qb[h], axis=0, keepdims=True)
            o_ref[h:h + 1, p * page:(p + 1) * page] = zrow + bias_ref[h:h + 1, :]


def _dec_w_kernel(z_ref, u_ref, ones_ref, a_ref):
    z = z_ref[...]
    sp = _softplus(z)
    lb = z - sp
    u = u_ref[...]
    ones = ones_ref[...]
    nblk = z.shape[1] // LANES
    run = jnp.zeros((z.shape[0], LANES), F32)
    for cblk in range(nblk - 1, -1, -1):
        spc = sp[:, cblk * LANES:(cblk + 1) * LANES]
        later = _split_dot(spc, u, 2) + run
        a_ref[:, cblk * LANES:(cblk + 1) * LANES] = jnp.exp(lb[:, cblk * LANES:(cblk + 1) * LANES] - later)
        run = run + _split_dot(spc, ones, 2)


def _dec_av_kernel(pt_ref, a_ref, *rest):
    v_refs, o_ref, acc_scr = rest[:-2], rest[-2], rest[-1]
    page = v_refs[0].shape[-1]

    @pl.when(pl.program_id(1) == 0)
    def _():
        acc_scr[...] = jnp.zeros_like(acc_scr)

    for h in range(SB_HEADS):
        acc = acc_scr[h]
        for p, v_ref in enumerate(v_refs):
            acc = acc + v_ref[h] * a_ref[h:h + 1, p * page:(p + 1) * page]
        acc_scr[h] = acc

    @pl.when(pl.program_id(1) == pl.num_programs(1) - 1)
    def _():
        lane = lax.broadcasted_iota(jnp.int32, (SB_HEAD_DIM, LANES), 1)
        out = jnp.zeros((SB_HEAD_DIM, LANES), F32)
        for h in range(SB_HEADS):
            out = jnp.where(lane == h, jnp.sum(acc_scr[h], axis=-1, keepdims=True), out)
        o_ref[...] = out


def sb_attention_decode(q, cache_k, cache_v, page_table, bias):
    nseq = q.shape[0]
    n_pages = page_table.shape[1]
    page = cache_k.shape[1]
    n_keys = n_pages * page
    pg = PAGES_PER_STEP
    kt = jnp.transpose(cache_k, (0, 2, 3, 1))
    vt = jnp.transpose(cache_v, (0, 2, 3, 1))
    qt = jnp.transpose(q.reshape(nseq, SB_HEADS, SB_HEAD_DIM), (0, 2, 1))
    bias_b = jnp.broadcast_to(bias.astype(F32).reshape(SB_HEADS, 1), (SB_HEADS, page))
    pt = page_table.reshape(-1).astype(jnp.int32)

    def page_spec(p):
        return pl.BlockSpec((None, SB_HEADS, SB_HEAD_DIM, page),
                            lambda s, g, pt: (pt[s * n_pages + g * pg + p], 0, 0, 0))

    z = pl.pallas_call(
        _dec_z_kernel,
        grid_spec=pltpu.PrefetchScalarGridSpec(
            num_scalar_prefetch=1,
            grid=(nseq, n_pages // pg),
            in_specs=[pl.BlockSpec((None, SB_HEAD_DIM, SB_HEADS), lambda s, g, pt: (s, 0, 0)),
                      pl.BlockSpec((SB_HEADS, page), lambda s, g, pt: (0, 0))]
                     + [page_spec(p) for p in range(pg)],
            out_specs=pl.BlockSpec((None, SB_HEADS, pg * page), lambda s, g, pt: (s, 0, g)),
        ),
        out_shape=jax.ShapeDtypeStruct((nseq, SB_HEADS, n_keys), F32),
        compiler_params=_params("parallel", "parallel"),
        name="sb_decode_logits",
    )(pt, qt, bias_b, *([kt] * pg))

    rows = 256
    li = lax.broadcasted_iota(jnp.int32, (LANES, LANES), 0)
    lj = lax.broadcasted_iota(jnp.int32, (LANES, LANES), 1)
    u = jnp.where(li > lj, 1.0, 0.0).astype(BF16)
    ones = jnp.ones((LANES, LANES), BF16)
    a = pl.pallas_call(
        _dec_w_kernel,
        grid=(nseq * SB_HEADS // rows,),
        in_specs=[pl.BlockSpec((rows, n_keys), lambda i: (i, 0)),
                  pl.BlockSpec((LANES, LANES), lambda i: (0, 0)),
                  pl.BlockSpec((LANES, LANES), lambda i: (0, 0))],
        out_specs=pl.BlockSpec((rows, n_keys), lambda i: (i, 0)),
        out_shape=jax.ShapeDtypeStruct((nseq * SB_HEADS, n_keys), F32),
        compiler_params=_params("parallel"),
        name="sb_decode_weights",
    )(z.reshape(nseq * SB_HEADS, n_keys), u, ones)

    o = pl.pallas_call(
        _dec_av_kernel,
        grid_spec=pltpu.PrefetchScalarGridSpec(
            num_scalar_prefetch=1,
            grid=(nseq, n_pages // pg),
            in_specs=[pl.BlockSpec((None, SB_HEADS, pg * page), lambda s, g, pt: (s, 0, g))]
                     + [page_spec(p) for p in range(pg)],
            out_specs=pl.BlockSpec((None, SB_HEAD_DIM, LANES), lambda s, g, pt: (s, 0, 0)),
            scratch_shapes=[pltpu.VMEM((SB_HEADS, SB_HEAD_DIM, page), F32)],
        ),
        out_shape=jax.ShapeDtypeStruct((nseq, SB_HEAD_DIM, LANES), F32),
        compiler_params=_params("parallel", "arbitrary"),
        name="sb_decode_values",
    )(pt, a.reshape(nseq, SB_HEADS, n_keys), *([vt] * pg))
    return jnp.transpose(o[:, :, :SB_HEADS], (0, 2, 1)).reshape(nseq, SB_HEADS * SB_HEAD_DIM)


def _router_kernel(x_ref, g_ref, sh_ref, sc_ref, wr_ref, br_ref,
                   h_ref, ti_ref, tw_ref, rk_ref, cnt_ref, carry_scr, *, tm):
    i = pl.program_id(0)

    @pl.when(i == 0)
    def _():
        carry_scr[...] = jnp.zeros_like(carry_scr)

    h = _normmod(x_ref[...], g_ref[...], sh_ref[...], sc_ref[...])
    h_ref[...] = h
    lane = lax.broadcasted_iota(jnp.int32, (tm, LANES), 1).astype(F32)
    logits = jnp.dot(h, wr_ref[...], precision=lax.Precision.HIGHEST,
                     preferred_element_type=F32) + br_ref[...]
    logits = jnp.where(lane < N_EXPERTS, logits, -jnp.inf)
    m1 = jnp.max(logits, axis=-1, keepdims=True)
    i1 = jnp.min(jnp.where(logits == m1, lane, float(LANES)), axis=-1, keepdims=True)
    rest = jnp.where(lane == i1, -jnp.inf, logits)
    m2 = jnp.max(rest, axis=-1, keepdims=True)
    i2 = jnp.min(jnp.where(rest == m2, lane, float(LANES)), axis=-1, keepdims=True)
    e2 = jnp.exp(m2 - m1)
    w1 = 1.0 / (1.0 + e2)
    w2 = e2 / (1.0 + e2)
    sel1 = lane == i1
    sel2 = lane == i2
    onehot = jnp.where(sel1, 1.0, jnp.where(sel2, 1.0, 0.0))
    r_i = lax.broadcasted_iota(jnp.int32, (tm, tm), 0)
    c_i = lax.broadcasted_iota(jnp.int32, (tm, tm), 1)
    before = jnp.where(c_i < r_i, 1.0, 0.0).astype(BF16)
    carry = carry_scr[0:1, :]
    rank = jnp.dot(before, onehot.astype(BF16), preferred_element_type=F32) + carry
    r1 = jnp.sum(jnp.where(sel1, rank, 0.0), axis=-1, keepdims=True)
    r2 = jnp.sum(jnp.where(sel2, rank, 0.0), axis=-1, keepdims=True)
    ti_ref[...] = jnp.where(lane == 0, i1, jnp.where(lane == 1, i2, 0.0)).astype(jnp.int32)
    tw_ref[...] = jnp.where(lane == 0, w1, jnp.where(lane == 1, w2, 0.0))
    rk_ref[...] = jnp.where(lane == 0, r1, jnp.where(lane == 1, r2, 0.0)).astype(jnp.int32)
    new_carry = carry + jnp.sum(onehot, axis=0, keepdims=True)
    carry_scr[...] = jnp.broadcast_to(new_carry, carry_scr.shape)
    cnt_ref[...] = jnp.broadcast_to(new_carry, cnt_ref.shape).astype(jnp.int32)


def moe_router(x, g, shift, scale, w_router, b_router, *, rows_per_seq, tm):
    t, d = x.shape
    wr = jnp.zeros((d, LANES), F32).at[:, :N_EXPERTS].set(w_router)
    br = jnp.zeros((1, LANES), F32).at[0, :N_EXPERTS].set(b_router)
    row = lambda i: (i, 0)
    return pl.pallas_call(
        functools.partial(_router_kernel, tm=tm),
        grid=(t // tm,),
        in_specs=[pl.BlockSpec((tm, d), row),
                  pl.BlockSpec((1, d), lambda i: (0, 0)),
                  _mod_spec(shift, rows_per_seq, tm),
                  _mod_spec(scale, rows_per_seq, tm),
                  pl.BlockSpec((d, LANES), lambda i: (0, 0)),
                  pl.BlockSpec((1, LANES), lambda i: (0, 0))],
        out_specs=[pl.BlockSpec((tm, d), row),
                   pl.BlockSpec((tm, LANES), row),
                   pl.BlockSpec((tm, LANES), row),
                   pl.BlockSpec((tm, LANES), row),
                   pl.BlockSpec((SUBLANES, LANES), lambda i: (0, 0))],
        out_shape=[jax.ShapeDtypeStruct((t, d), F32),
                   jax.ShapeDtypeStruct((t, LANES), jnp.int32),
                   jax.ShapeDtypeStruct((t, LANES), F32),
                   jax.ShapeDtypeStruct((t, LANES), jnp.int32),
                   jax.ShapeDtypeStruct((SUBLANES, LANES), jnp.int32)],
        scratch_shapes=[pltpu.VMEM((SUBLANES, LANES), F32)],
        compiler_params=_params("arbitrary"),
        name="moe_router",
    )(x, g.reshape(1, d), shift, scale, wr, br)


def _row_copy_kernel(idx_ref, src_ref, *rest, rows_per_step, n_src, scatter):
    out_ref, sem = rest[-2], rest[-1]
    base = pl.program_id(0) * rows_per_step

    def copy(i):
        if scatter:
            s = jnp.where(i >= n_src, i - n_src, i)
            d = idx_ref[i]
        else:
            s = idx_ref[i]
            d = i
        return pltpu.make_async_copy(src_ref.at[pl.ds(s, 1)], out_ref.at[pl.ds(d, 1)], sem)

    def start(r, carry):
        copy(base + r).start()
        return carry

    lax.fori_loop(0, rows_per_step, start, 0)

    def wait(r, carry):
        copy(base + r).wait()
        return carry

    lax.fori_loop(0, rows_per_step, wait, 0)


def row_copy(src, idx, n_out, *, scatter, rows_per_step):
    n_src, d = src.shape
    n = idx.shape[0]
    extra = [jnp.zeros((n_out, d), src.dtype)] if scatter else []
    return pl.pallas_call(
        functools.partial(_row_copy_kernel, rows_per_step=rows_per_step, n_src=n_src, scatter=scatter),
        grid_spec=pltpu.PrefetchScalarGridSpec(
            num_scalar_prefetch=1,
            grid=(n // rows_per_step,),
            in_specs=[pl.BlockSpec(memory_space=pl.ANY)] * (1 + len(extra)),
            out_specs=pl.BlockSpec(memory_space=pl.ANY),
            scratch_shapes=[pltpu.SemaphoreType.DMA(())],
        ),
        out_shape=jax.ShapeDtypeStruct((n_out, d), src.dtype),
        input_output_aliases={2: 0} if scatter else {},
        compiler_params=_params("arbitrary"),
        name="row_scatter" if scatter else "row_gather",
    )(idx, src, *extra)


def _moe_ffn_kernel(te_ref, nu_ref, x_ref, wg_ref, wu_ref, wd_ref, o_ref, acc_scr):
    i = pl.program_id(0)
    f = pl.program_id(1)
    used = i < nu_ref[0]

    @pl.when(f == 0)
    def _():
        acc_scr[...] = jnp.zeros_like(acc_scr)

    @pl.when(used)
    def _():
        h = x_ref[...].astype(BF16)
        a = jnp.dot(h, wg_ref[...].astype(BF16), preferred_element_type=F32)
        b = jnp.dot(h, wu_ref[...].astype(BF16), preferred_element_type=F32)
        acc_scr[...] += _bdot(_silu(a) * b, wd_ref[...])

    @pl.when(f == pl.num_programs(1) - 1)
    def _():
        o_ref[...] = acc_scr[...]


def moe_ffn(xs, tile_expert, n_used, w_gate_up, w_down, *, tm, tf):
    p, d = xs.shape
    ff = w_down.shape[1]
    nf = ff // tf
    n_tiles = p // tm

    def tile(i, nu):
        return jnp.minimum(i, nu[0] - 1)

    def fblk(i, f, nu):
        return jnp.where(i < nu[0], f, nf - 1)

    return pl.pallas_call(
        _moe_ffn_kernel,
        grid_spec=pltpu.PrefetchScalarGridSpec(
            num_scalar_prefetch=2,
            grid=(n_tiles, nf),
            in_specs=[pl.BlockSpec((tm, d), lambda i, f, te, nu: (tile(i, nu), 0)),
                      pl.BlockSpec((None, d, tf), lambda i, f, te, nu: (te[tile(i, nu)], 0, fblk(i, f, nu))),
                      pl.BlockSpec((None, d, tf), lambda i, f, te, nu: (te[tile(i, nu)], 0, nf + fblk(i, f, nu))),
                      pl.BlockSpec((None, tf, d), lambda i, f, te, nu: (te[tile(i, nu)], fblk(i, f, nu), 0))],
            out_specs=pl.BlockSpec((tm, d), lambda i, f, te, nu: (i, 0)),
            scratch_shapes=[pltpu.VMEM((tm, d), F32)],
        ),
        out_shape=jax.ShapeDtypeStruct((p, d), F32),
        compiler_params=_params("arbitrary", "arbitrary"),
        name="moe_ffn",
    )(tile_expert, n_used, xs, w_gate_up, w_gate_up, w_down)


def _combine_kernel(x_ref, gt_ref, w_ref, y0_ref, y1_ref, o_ref):
    w = w_ref[...]
    o_ref[...] = x_ref[...] + gt_ref[...] * (w[:, 0:1] * y0_ref[...] + w[:, 1:2] * y1_ref[...])


def moe_combine(x, gate, top_w, yg, *, rows_per_seq, tm):
    t, d = x.shape
    nt = t // tm
    return pl.pallas_call(
        _combine_kernel,
        grid=(nt,),
        in_specs=[pl.BlockSpec((tm, d), lambda i: (i, 0)),
                  _mod_spec(gate, rows_per_seq, tm),
                  pl.BlockSpec((tm, LANES), lambda i: (i, 0)),
                  pl.BlockSpec((tm, d), lambda i: (i, 0)),
                  pl.BlockSpec((tm, d), lambda i: (nt + i, 0))],
        out_specs=pl.BlockSpec((tm, d), lambda i: (i, 0)),
        out_shape=jax.ShapeDtypeStruct((t, d), F32),
        compiler_params=_params("parallel"),
        name="moe_combine",
    )(x, gate, top_w, yg, yg)


def moe_block(x, g, shift, scale, gate, w_router, b_router, w_gate_up, w_down, *,
              rows_per_seq, tm, tm_group, tf, rows_per_step):
    t, d = x.shape
    h, top_i, top_w, rank, cnt = moe_router(x, g, shift, scale, w_router, b_router,
                                            rows_per_seq=rows_per_seq, tm=tm)
    counts = cnt[0, :N_EXPERTS]
    tiles_e = (counts + tm_group - 1) // tm_group
    tile_end = jnp.cumsum(tiles_e)
    offs = (tile_end - tiles_e) * tm_group
    sel = top_i[:, :2]
    pos = (offs[sel] + rank[:, :2]).T.reshape(-1).astype(jnp.int32)
    n_tiles = -(-2 * t // tm_group) + N_EXPERTS
    n_used = tile_end[-1:].astype(jnp.int32)
    tile_expert = jnp.minimum(
        jnp.searchsorted(tile_end, jnp.arange(n_tiles, dtype=tile_end.dtype), side="right"),
        N_EXPERTS - 1).astype(jnp.int32)
    xs = row_copy(h, pos, n_tiles * tm_group, scatter=True, rows_per_step=rows_per_step)
    ys = moe_ffn(xs, tile_expert, n_used, w_gate_up, w_down, tm=tm_group, tf=tf)
    yg = row_copy(ys, pos, 2 * t, scatter=False, rows_per_step=rows_per_step)
    return moe_combine(x, gate, top_w, yg, rows_per_seq=rows_per_seq, tm=tm)


def _trunk(x, mods, kv_mod, fin_mod, gla_s0, cache, p, cfg):
    d = x.shape[1]
    kw = dict(rows_per_seq=cfg["length"], tm=cfg["tm"])
    n_a = p["gla_w_in"].shape[0]
    depth = p["ada_w"].shape[0]
    states = []
    k_new = v_new = None
    for layer in range(depth):
        sh1, sc1, g1, sh2, sc2, g2 = mods[layer]
        if layer < n_a:
            w_in = p["gla_w_in"][layer]
            n_in = w_in.shape[1]
            n_pad = -(-n_in // 640) * 640
            w_in = jnp.zeros((d, n_pad), F32).at[:, :n_in].set(w_in)
            proj = normmod_matmul(x, p["norm_mix"][layer], sh1, sc1, w_in, tn=640, **kw)
            lp = cfg["gla_len"]
            proj = proj.reshape(cfg["bsz"], cfg["length"], n_pad)
            if lp != cfg["length"]:
                proj = jnp.zeros((cfg["bsz"], lp, n_pad), F32).at[:, :cfg["length"]].set(proj)
            y, s_new = gla_scan(proj, p["gla_w_gate2"][layer], p["gla_b_gate"][layer], p["gla_norm"][layer],
                                None if gla_s0 is None else gla_s0[layer],
                                chunk=cfg["gla_chunk"], n_chunks=cfg["gla_nchunks"],
                                valid_len=None if lp == cfg["length"] else cfg["length"])
            states.append(s_new)
            y = y[:, :cfg["length"]].reshape(-1, d)
            x = matmul_residual(y, p["gla_w_out"][layer], x, g1, tn=d, **kw)
        else:
            bi = layer - n_a
            q = normmod_matmul(x, p["norm_mix"][layer], sh1, sc1, p["sb_w_q"][bi], tn=512,
                               out_scale=float(SB_HEAD_DIM) ** -0.5, **kw)
            if cache is None:
                o = sb_attention_prompt(q.reshape(cfg["bsz"], cfg["length"], d),
                                        k_new.reshape(cfg["bsz"], cfg["length"], d),
                                        v_new.reshape(cfg["bsz"], cfg["length"], d),
                                        p["sb_b"][bi], tq=cfg["sb_tq"]).reshape(-1, d)
            else:
                o = sb_attention_decode(q, cache[0], cache[1], cache[2], p["sb_b"][bi])
            x = matmul_residual(o, p["sb_w_out"][bi], x, g1, tn=d, **kw)
        if layer % 2 == 0:
            x = ffn_block(x, p["norm_ffn"][layer], sh2, sc2, g2, p["ffn_w_gate_up"][layer // 2],
                          p["ffn_w_down"][layer // 2], rows_per_seq=cfg["length"], tm=cfg["tm_ffn"], tf=256)
        else:
            x = moe_block(x, p["norm_ffn"][layer], sh2, sc2, g2, p["moe_w_router"][layer // 2],
                          p["moe_b_router"][layer // 2], p["moe_w_gate_up"][layer // 2],
                          p["moe_w_down"][layer // 2], rows_per_seq=cfg["length"], tm=cfg["tm"],
                          tm_group=cfg["tm_group"], tf=512, rows_per_step=cfg["rows_per_step"])
        if layer == n_a - 1:
            hd = (SB_HEADS, SB_HEAD_DIM)
            if cache is None:
                k_new, kt = normmod_matmul_t(x, p["kv_norm"], kv_mod[0], kv_mod[1], p["kv_w"][:, :d], tn=512, **kw)
                v_new, vt = normmod_matmul_t(x, p["kv_norm"], kv_mod[0], kv_mod[1], p["kv_w"][:, d:], tn=512, **kw)
                k_out = jnp.transpose(kt.reshape(cfg["bsz"], *hd, cfg["length"]), (0, 3, 1, 2))
                v_out = jnp.transpose(vt.reshape(cfg["bsz"], *hd, cfg["length"]), (0, 3, 1, 2))
            else:
                kv = normmod_matmul(x, p["kv_norm"], kv_mod[0], kv_mod[1], p["kv_w"], tn=512, **kw)
                k_out = kv[:, :d].reshape(cfg["bsz"], cfg["length"], *hd)
                v_out = kv[:, d:].reshape(cfg["bsz"], cfg["length"], *hd)
    y = normmod(x, p["final_norm"], fin_mod[0], fin_mod[1], **kw)
    return y, k_out, v_out, jnp.stack(states)


def kernel(x_prompt, x_sample, cache_k, cache_v, state_gla, page_table, c_prompt, c_sample, ada_w, ada_b, norm_mix, norm_ffn, gla_w_in, gla_w_gate2, gla_b_gate, gla_norm, gla_w_out, kv_norm, kv_ada_w, kv_ada_b, kv_w, sb_w_q, sb_b, sb_w_out, ffn_w_gate_up, ffn_w_down, moe_w_router, moe_b_router, moe_w_gate_up, moe_w_down, final_norm, final_ada_w, final_ada_b):
    p = dict(ada_w=ada_w, norm_mix=norm_mix, norm_ffn=norm_ffn, gla_w_in=gla_w_in, gla_w_gate2=gla_w_gate2,
             gla_b_gate=gla_b_gate, gla_norm=gla_norm, gla_w_out=gla_w_out, kv_norm=kv_norm, kv_w=kv_w,
             sb_w_q=sb_w_q, sb_b=sb_b, sb_w_out=sb_w_out, ffn_w_gate_up=ffn_w_gate_up, ffn_w_down=ffn_w_down,
             moe_w_router=moe_w_router, moe_b_router=moe_b_router, moe_w_gate_up=moe_w_gate_up,
             moe_w_down=moe_w_down, final_norm=final_norm)
    b_p, seq, d = x_prompt.shape
    b_s, dec_seq, _ = x_sample.shape
    depth = ada_w.shape[0]

    c_all = jnp.concatenate([c_prompt, c_sample], axis=0)
    mod = ada_matmul(c_all, ada_w, ada_b)
    mod2 = ada_matmul(c_all, jnp.stack([kv_ada_w, final_ada_w]), jnp.stack([kv_ada_b, final_ada_b]))

    def split(m, n, lo, hi, per_seq):
        parts = [m[lo:hi, i * d:(i + 1) * d] for i in range(n)]
        if per_seq:
            return [a.reshape(hi - lo, 1, d) for a in parts]
        return [a.reshape(1, hi - lo, d) for a in parts]

    outs = []
    for lo, hi, per_seq in ((0, b_p, True), (b_p, b_p + b_s, False)):
        mods = [split(mod[l], 6, lo, hi, per_seq) for l in range(depth)]
        kv_mod = split(mod2[0], 2, lo, hi, per_seq)
        fin_mod = split(mod2[1], 2, lo, hi, per_seq)
        if per_seq:
            cfg = dict(bsz=b_p, length=seq, tm=512, tm_ffn=min(1024, seq), tm_group=512, rows_per_step=512,
                       gla_len=seq, gla_chunk=GLA_CHUNK, gla_nchunks=4, sb_tq=256)
            outs.append(_trunk(x_prompt.reshape(b_p * seq, d), mods, kv_mod, fin_mod, None, None, p, cfg))
        else:
            cfg = dict(bsz=b_s, length=dec_seq, tm=b_s, tm_ffn=b_s, tm_group=128, rows_per_step=2 * b_s,
                       gla_len=GLA_SUB, gla_chunk=GLA_SUB, gla_nchunks=1, sb_tq=None)
            outs.append(_trunk(x_sample.reshape(b_s * dec_seq, d), mods, kv_mod, fin_mod, state_gla,
                               (cache_k, cache_v, page_table), p, cfg))
    (y_p, k_p, v_p, s_p), (y_s, k_s, v_s, s_s) = outs
    return (y_p.reshape(b_p, seq, d), y_s.reshape(b_s, dec_seq, d), k_p, v_p, s_p, k_s, v_s, s_s)
```
